```python
import math
import jax, jax.numpy as jnp
from jax import lax
import numpy as np

D_MODEL = 2048
BATCH = 4
SEQ = 4096
DEPTH = 1
DEC_BATCH = 2
DEC_SEQ = 8192
PAST_LEN = 128

P_DIM = 256
RET_HEADS = 8
RET_DK = 128
RET_DV = 256
RET_CHUNK = 128
ROPE_BASE = 10000.0
ATT_GROUPS = ((128, 1), (512, 4), (2048, 16))
ATT_HEADS = 8
ATT_DH = 128
N_ATT_HEADS = ATT_HEADS * len(ATT_GROUPS)
NEG_INF = -1e30
NUM_BUCKETS = 32
REL_MAX_DIST = 1024
PEER_HEADS = 8
PEER_NKEYS = 128
PEER_N = PEER_NKEYS * PEER_NKEYS
PEER_QDIM = 256
PEER_QHALF = PEER_QDIM // 2
PEER_TOPK = 16
PEER_BLOCK = 64
DN_ALPHA = (2.0 * DEPTH) ** 0.25
DN_BETA = (8.0 * DEPTH) ** -0.25
LN_EPS = 1e-5

RET_QK_W = RET_HEADS * RET_DK
RET_V_W = RET_HEADS * RET_DV
ATT_W = N_ATT_HEADS * ATT_DH
ATT_OUT_W = ATT_HEADS * ATT_DH
SPLITS = (RET_QK_W, RET_QK_W, RET_V_W, RET_V_W, ATT_W, ATT_W, ATT_W, D_MODEL, D_MODEL)
SPLIT_IDX = tuple(int(c) for c in np.cumsum(SPLITS)[:-1])
IN_W = int(sum(SPLITS))

kernel_name = "hybrid_retention_dilated_peer_encoder"

f32 = jnp.float32


def _layernorm(x, g, b):
    xf = x.astype(f32)
    mu = xf.mean(-1, keepdims=True)
    var = jnp.square(xf - mu).mean(-1, keepdims=True)
    y = (xf - mu) * lax.rsqrt(var + LN_EPS) * g.astype(f32) + b.astype(f32)
    return y.astype(x.dtype)


def _rotary(t):
    S = t.shape[1]
    half = t.shape[-1] // 2
    inv = 1.0 / (ROPE_BASE ** jnp.linspace(0.0, 1.0, half, dtype=f32))
    ang = jnp.arange(S, dtype=f32)[:, None] * inv[None, :]
    cos = jnp.cos(ang)[None, :, None, :].astype(t.dtype)
    sin = jnp.sin(ang)[None, :, None, :].astype(t.dtype)
    t1, t2 = t[..., :half], t[..., half:]
    return jnp.concatenate([t1 * cos - t2 * sin, t1 * sin + t2 * cos], axis=-1)


def _retention_dir(q, k, v, log_gamma, strict):
    B, H, S, dk = q.shape
    dv = v.shape[-1]
    n = S // RET_CHUNK
    qc = q.reshape(B, H, n, RET_CHUNK, dk)
    kc = k.reshape(B, H, n, RET_CHUNK, dk)
    vc = v.reshape(B, H, n, RET_CHUNK, dv)
    idx = jnp.arange(RET_CHUNK, dtype=f32)
    diff = idx[:, None] - idx[None, :]
    mask = (diff > 0) if strict else (diff >= 0)
    decay = jnp.where(mask[None], jnp.exp(log_gamma[:, None, None] * jnp.maximum(diff, 0.0)[None]), 0.0)
    scores = jnp.einsum('bhnid,bhnjd->bhnij', qc, kc) * decay[None, :, None]
    o_intra = jnp.einsum('bhnij,bhnjv->bhniv', scores, vc)
    zeta = jnp.exp(log_gamma[:, None] * (RET_CHUNK - 1 - idx)[None])
    kv = jnp.einsum('bhnjd,hj,bhnjv->bhndv', kc, zeta, vc)
    chunk_decay = jnp.exp(log_gamma * RET_CHUNK).astype(kv.dtype)[None, :, None, None]

    def step(R, kv_c):
        return R * chunk_decay + kv_c, R

    _, R_prev = lax.scan(step, jnp.zeros((B, H, dk, dv), kv.dtype), jnp.moveaxis(kv, 2, 0))
    R_prev = jnp.moveaxis(R_prev, 0, 2)
    xi = jnp.exp(log_gamma[:, None] * (idx + 1.0)[None])
    o_cross = jnp.einsum('bhnid,hi,bhndv->bhniv', qc, xi, R_prev)
    return (o_intra + o_cross).reshape(B, H, S, dv)


def _retention_branch(rq, rk, rv, rg, decay_logit):
    B, S, _ = rq.shape
    q = _rotary(rq.reshape(B, S, RET_HEADS, RET_DK))
    k = _rotary(rk.reshape(B, S, RET_HEADS, RET_DK)) * (RET_DK ** -0.5)
    v = rv.reshape(B, S, RET_HEADS, RET_DV)
    q, k, v = [t.transpose(0, 2, 1, 3) for t in (q, k, v)]
    log_gamma = jax.nn.log_sigmoid(decay_logit.astype(f32))
    flip = lambda t: jnp.flip(t, axis=2)
    fwd = _retention_dir(q, k, v, log_gamma[0], False)
    bwd = flip(_retention_dir(flip(q), flip(k), flip(v), log_gamma[1], True))
    o = (fwd + bwd).astype(f32)
    mu = o.mean(-1, keepdims=True)
    var = jnp.square(o - mu).mean(-1, keepdims=True)
    o = (o - mu) * lax.rsqrt(var + LN_EPS)
    o = o.transpose(0, 2, 1, 3).reshape(B, S, RET_V_W).astype(rq.dtype)
    return o * jax.nn.silu(rg)


def _t5_bucket(rel):
    nb = NUM_BUCKETS // 2
    max_exact = nb // 2
    ret = jnp.where(rel > 0, nb, 0)
    n = jnp.abs(rel)
    nf = jnp.maximum(n, 1).astype(f32)
    large = max_exact + (jnp.log(nf / max_exact) / math.log(REL_MAX_DIST / max_exact)
                         * (nb - max_exact)).astype(jnp.int32)
    large = jnp.minimum(large, nb - 1)
    return ret + jnp.where(n < max_exact, n, large)


def _dilated_group(q, k, v, bias_tab, window, dil):
    B, S, H, d = q.shape
    half = window // (2 * dil)
    blk = half
    L = S // dil
    nblk = -(-L // blk)
    Lp = nblk * blk
    pad = Lp - L
    z = B * dil

    def to_res(t):
        return t.reshape(B, L, dil, H, d).transpose(0, 2, 1, 3, 4).reshape(z, L, H, d)

    qb = jnp.pad(to_res(q) * (d ** -0.5), ((0, 0), (0, pad), (0, 0), (0, 0))).reshape(z, nblk, blk, H, d)

    def windows(t):
        tp = jnp.pad(to_res(t), ((0, 0), (blk, pad + blk), (0, 0), (0, 0))).reshape(z, nblk + 2, blk, H, d)
        return jnp.concatenate([tp[:, :-2], tp[:, 1:-1], tp[:, 2:]], axis=2)

    kw, vw = windows(k), windows(v)
    i = jnp.arange(blk)[:, None]
    j = jnp.arange(3 * blk)[None, :]
    off = j - blk - i
    kpos = jnp.arange(nblk)[:, None, None] * blk + (j - blk)[None]
    valid = (jnp.abs(off) <= half)[None] & (kpos >= 0) & (kpos < L)
    bias = bias_tab[_t5_bucket(off * dil)].astype(f32).transpose(2, 0, 1)
    logits = jnp.einsum('znihd,znjhd->znhij', qb, kw).astype(f32) + bias
    logits = jnp.where(valid[None, :, None], logits, NEG_INF)
    m = logits.max(-1, keepdims=True)
    p = jnp.exp(logits - m)
    den = p.sum(-1)
    o = jnp.einsum('znhij,znjhd->znihd', p, vw.astype(f32)) / jnp.swapaxes(den, 2, 3)[..., None]
    lse = jnp.swapaxes(m[..., 0] + jnp.log(den), 2, 3)
    o = o.reshape(z, Lp, H, d)[:, :L].reshape(B, dil, L, H, d).transpose(0, 2, 1, 3, 4).reshape(B, S, H, d)
    lse = lse.reshape(z, Lp, H)[:, :L].reshape(B, dil, L, H).transpose(0, 2, 1, 3).reshape(B, S, H)
    return o, lse


def _dilated_branch(aq, ak, av, rel_bias):
    B, S, _ = aq.shape
    shp = (B, S, len(ATT_GROUPS), ATT_HEADS, ATT_DH)
    q, k, v = aq.reshape(shp), ak.reshape(shp), av.reshape(shp)
    outs, lses = [], []
    for gi, (w, r) in enumerate(ATT_GROUPS):
        o, l = _dilated_group(q[:, :, gi], k[:, :, gi], v[:, :, gi],
                              rel_bias[:, gi * ATT_HEADS:(gi + 1) * ATT_HEADS], w, r)
        outs.append(o)
        lses.append(l)
    wts = jax.nn.softmax(jnp.stack(lses, 0), axis=0)
    o = jnp.sum(wts[..., None] * jnp.stack(outs, 0), axis=0)
    return o.reshape(B, S, ATT_OUT_W).astype(aq.dtype)


def _peer(x, wq, keys, u_tab, v_tab):
    B, S, D = x.shape
    T = B * S
    xt = x.reshape(T, D)
    q = (xt @ wq).reshape(T, PEER_HEADS, 2, PEER_QHALF)
    s = jnp.einsum('thcd,hckd->thck', q, keys).astype(f32)
    s_top, i_top = lax.top_k(s, PEER_TOPK)
    cand = s_top[:, :, 0, :, None] + s_top[:, :, 1, None, :]
    cand_idx = i_top[:, :, 0, :, None] * PEER_NKEYS + i_top[:, :, 1, None, :]
    best, pos = lax.top_k(cand.reshape(T, PEER_HEADS, PEER_TOPK * PEER_TOPK), PEER_TOPK)
    eidx = jnp.take_along_axis(cand_idx.reshape(T, PEER_HEADS, PEER_TOPK * PEER_TOPK), pos, axis=-1)
    g = jax.nn.softmax(best, axis=-1).astype(x.dtype)
    nb = T // PEER_BLOCK

    def blk(args):
        xb, eb, gb = args
        u = u_tab[eb]
        h = jax.nn.gelu(jnp.einsum('td,thkd->thk', xb, u), approximate=False)
        vv = v_tab[eb]
        return jnp.einsum('thk,thkd->td', gb * h, vv)

    out = lax.map(blk, (xt.reshape(nb, PEER_BLOCK, D),
                        eidx.reshape(nb, PEER_BLOCK, PEER_HEADS, PEER_TOPK),
                        g.reshape(nb, PEER_BLOCK, PEER_HEADS, PEER_TOPK)))
    return out.reshape(B, S, D).astype(x.dtype)


def _layer(x, p, w_in, ret_decay_logit, w_ret_o, w_att_o, w_out, rel_bias, ln1_g, ln1_b,
           peer_wq, peer_keys, peer_u, peer_v, w_pe, w_pg, ln2_g, ln2_b):
    proj = x @ w_in
    rq, rk, rv, rg, aq, ak, av, ga, gb = jnp.split(proj, SPLIT_IDX, axis=-1)
    ret = _retention_branch(rq, rk, rv, rg, ret_decay_logit)
    att = _dilated_branch(aq, ak, av, rel_bias)
    merged = jax.nn.sigmoid(ga) * (ret @ w_ret_o) + jax.nn.sigmoid(gb) * (att @ w_att_o)
    h = _layernorm(DN_ALPHA * x + merged @ w_out, ln1_g, ln1_b)
    ff = _peer(h, peer_wq, peer_keys, peer_u, peer_v)
    pe = (p @ w_pe) * jax.nn.sigmoid(h @ w_pg)
    return _layernorm(DN_ALPHA * h + ff + pe, ln2_g, ln2_b)


def setup_inputs(seed: int = 0) -> dict:
    key = jax.random.key(seed)
    ks = jax.random.split(key, 24)
    nrm = lambda k, shape, s: jax.random.normal(k, shape, f32) * s
    x_prompt = nrm(ks[0], (BATCH, SEQ, D_MODEL), 1.0)
    x_sample = nrm(ks[1], (DEC_BATCH, DEC_SEQ, D_MODEL), 1.0)
    p_prompt = nrm(ks[2], (DEPTH, BATCH, SEQ, P_DIM), 1.0)
    p_sample = nrm(ks[3], (DEPTH, DEC_BATCH, DEC_SEQ, P_DIM), 1.0)
    col_scale = jnp.concatenate([jnp.full((n,), s, f32) for n, s in
                                 zip(SPLITS, (1.0, 1.0, DN_BETA, 1.0, 1.0, 1.0, DN_BETA, 1.0, 1.0))])
    w_in = nrm(ks[4], (DEPTH, D_MODEL, IN_W), D_MODEL ** -0.5) * col_scale
    base = jnp.log(2.0 ** (5.0 + jnp.arange(RET_HEADS, dtype=f32)) - 1.0)
    ret_decay_logit = base[None, None, :] + nrm(ks[5], (DEPTH, 2, RET_HEADS), 0.05)
    w_ret_o = nrm(ks[6], (DEPTH, RET_V_W, D_MODEL), DN_BETA * RET_V_W ** -0.5)
    w_att_o = nrm(ks[7], (DEPTH, ATT_OUT_W, D_MODEL), DN_BETA * ATT_OUT_W ** -0.5)
    w_out = nrm(ks[8], (DEPTH, D_MODEL, D_MODEL), DN_BETA * D_MODEL ** -0.5)
    rel_bias = nrm(ks[9], (NUM_BUCKETS, N_ATT_HEADS), 0.2)
    ln1_g = 1.0 + nrm(ks[10], (DEPTH, D_MODEL), 0.02)
    ln1_b = nrm(ks[11], (DEPTH, D_MODEL), 0.02)
    peer_wq = nrm(ks[12], (DEPTH, D_MODEL, PEER_HEADS * PEER_QDIM), D_MODEL ** -0.5)
    peer_keys = nrm(ks[13], (DEPTH, PEER_HEADS, 2, PEER_NKEYS, PEER_QHALF), PEER_QHALF ** -0.5)
    peer_u = nrm(ks[14], (DEPTH, PEER_N, D_MODEL), D_MODEL ** -0.5)
    peer_v = nrm(ks[15], (DEPTH, PEER_N, D_MODEL), DN_BETA * PEER_HEADS ** -0.5)
    w_pe = nrm(ks[16], (DEPTH, P_DIM, D_MODEL), DN_BETA * P_DIM ** -0.5)
    w_pg = nrm(ks[17], (DEPTH, D_MODEL, D_MODEL), D_MODEL ** -0.5)
    ln2_g = 1.0 + nrm(ks[18], (DEPTH, D_MODEL), 0.02)
    ln2_b = nrm(ks[19], (DEPTH, D_MODEL), 0.02)
    return {"x_prompt": x_prompt, "x_sample": x_sample, "p_prompt": p_prompt, "p_sample": p_sample,
            "w_in": w_in, "ret_decay_logit": ret_decay_logit, "w_ret_o": w_ret_o, "w_att_o": w_att_o,
            "w_out": w_out, "rel_bias": rel_bias, "ln1_g": ln1_g, "ln1_b": ln1_b,
            "peer_wq": peer_wq, "peer_keys": peer_keys, "peer_u": peer_u, "peer_v": peer_v,
            "w_pe": w_pe, "w_pg": w_pg, "ln2_g": ln2_g, "ln2_b": ln2_b}


def reference(x_prompt, x_sample, p_prompt, p_sample, w_in, ret_decay_logit, w_ret_o, w_att_o,
              w_out, rel_bias, ln1_g, ln1_b, peer_wq, peer_keys, peer_u, peer_v, w_pe, w_pg,
              ln2_g, ln2_b):
    y_prompt = x_prompt
    y_sample = x_sample
    for i in range(DEPTH):
        y_prompt = _layer(y_prompt, p_prompt[i], w_in[i], ret_decay_logit[i], w_ret_o[i], w_att_o[i],
                          w_out[i], rel_bias, ln1_g[i], ln1_b[i], peer_wq[i], peer_keys[i], peer_u[i],
                          peer_v[i], w_pe[i], w_pg[i], ln2_g[i], ln2_b[i])
        y_sample = _layer(y_sample, p_sample[i], w_in[i], ret_decay_logit[i], w_ret_o[i], w_att_o[i],
                          w_out[i], rel_bias, ln1_g[i], ln1_b[i], peer_wq[i], peer_keys[i], peer_u[i],
                          peer_v[i], w_pe[i], w_pg[i], ln2_g[i], ln2_b[i])
    return (y_prompt, y_sample)
```

```python
import functools
import math

import jax
import jax.numpy as jnp
from jax import lax
from jax.experimental import pallas as pl
from jax.experimental.pallas import tpu as pltpu

f32 = jnp.float32
bf16 = jnp.bfloat16

D_MODEL = 2048
DEPTH = 1
RET_HEADS = 8
RET_DK = 128
RET_DV = 256
RET_CHUNK = 128
ROPE_BASE = 10000.0
ATT_GROUPS = ((128, 1), (512, 4), (2048, 16))
ATT_HEADS = 8
ATT_DH = 128
ATT_HALF = 64
NUM_BUCKETS = 32
REL_MAX_DIST = 1024
PEER_HEADS = 8
PEER_NKEYS = 128
PEER_TOPK = 16
P_DIM = 256
DN_ALPHA = (2.0 * DEPTH) ** 0.25
LN_EPS = 1e-5
NEG_INF = -1e30

COL_GA, COL_GB, COL_RV, COL_RG, COL_RQ, COL_RK, COL_AQ, COL_AK, COL_AV = (
    0, 2048, 4096, 6144, 8192, 9216, 10240, 13312, 16384)
IN_W = 19456

VMEM_LIMIT = 56 * 1024 * 1024


def _cparams(sem):
    return pltpu.CompilerParams(dimension_semantics=sem, vmem_limit_bytes=VMEM_LIMIT)


def _const_spec(shape):
    nd = len(shape)
    return pl.BlockSpec(shape, lambda *_: (0,) * nd, pipeline_mode=pl.Buffered(1))


def _layernorm(y, g, b):
    mu = jnp.mean(y, axis=-1, keepdims=True)
    d = y - mu
    var = jnp.mean(d * d, axis=-1, keepdims=True)
    return d * lax.rsqrt(var + LN_EPS) * g + b


def _proj_kernel(x_ref, w_ref, o_ref):
    o_ref[...] = jnp.dot(x_ref[...].astype(bf16), w_ref[...],
                         preferred_element_type=f32).astype(o_ref.dtype)


def _proj(x2, w, tm=1024, tn=1024):
    M, K = x2.shape
    N = w.shape[1]
    return pl.pallas_call(
        _proj_kernel,
        grid=(M // tm, N // tn),
        in_specs=[pl.BlockSpec((tm, K), lambda i, j: (i, 0)),
                  pl.BlockSpec((K, tn), lambda i, j: (0, j))],
        out_specs=pl.BlockSpec((tm, tn), lambda i, j: (i, j)),
        out_shape=jax.ShapeDtypeStruct((M, N), bf16),
        compiler_params=_cparams(("parallel", "arbitrary")),
        name="proj",
    )(x2, w)


def _rot(t, cos, sin):
    return t * cos + pltpu.roll(t, RET_DK // 2, 1) * sin


def _ret_kernel(lg_ref, q_ref, k_ref, v_ref, g_ref, cos_ref, sin_ref, o_ref, acc_ref, *, n_chunks):
    C = RET_CHUNK
    h = pl.program_id(1)
    lgf = lg_ref[0, h]
    lgb = lg_ref[1, h]
    row = lax.broadcasted_iota(jnp.int32, (C, C), 0).astype(f32)
    col = lax.broadcasted_iota(jnp.int32, (C, C), 1).astype(f32)
    diff = row - col
    dec_f = jnp.where(diff >= 0, jnp.exp(lgf * jnp.maximum(diff, 0.0)), 0.0)
    dec_b = jnp.where(diff < 0, jnp.exp(lgb * jnp.maximum(-diff, 0.0)), 0.0)
    xi_f = jnp.exp(lgf * (row + 1.0))
    zeta_f = jnp.exp(lgf * (C - 1.0 - row))
    xi_b = jnp.exp(lgb * (C - row))
    zeta_b = jnp.exp(lgb * row)
    cd_f = jnp.exp(jnp.full((1, RET_DV), lgf * C, f32))
    cd_b = jnp.exp(jnp.full((1, RET_DV), lgb * C, f32))
    scale = RET_DK ** -0.5
    nt = (((1,), (1,)), ((), ()))

    def chunk(n, R, dec, xi, zeta):
        sl = pl.ds(pl.multiple_of(n * C, C), C)
        cos = cos_ref[sl, :]
        sin = sin_ref[sl, :]
        q = _rot(q_ref[sl, :].astype(f32), cos, sin)
        k = _rot(k_ref[sl, :].astype(f32), cos, sin) * scale
        v = v_ref[sl, :]
        s = lax.dot_general(q.astype(bf16), k.astype(bf16), nt, preferred_element_type=f32) * dec
        o = jnp.dot(s.astype(bf16), v, preferred_element_type=f32)
        o = o + jnp.dot((q * xi).astype(bf16), R.astype(bf16), preferred_element_type=f32)
        kz = (k * zeta).T.astype(bf16)
        kv = jnp.dot(kz, v, preferred_element_type=f32)
        return sl, o, kv

    def fwd(n, R):
        sl, o, kv = chunk(n, R, dec_f, xi_f, zeta_f)
        acc_ref[sl, :] = o
        return R * cd_f + kv

    R0 = jnp.zeros((RET_DK, RET_DV), f32)
    lax.fori_loop(0, n_chunks, fwd, R0)

    def bwd(i, R):
        n = n_chunks - 1 - i
        sl, o, kv = chunk(n, R, dec_b, xi_b, zeta_b)
        o = o + acc_ref[sl, :]
        mu = jnp.mean(o, axis=-1, keepdims=True)
        d = o - mu
        var = jnp.mean(d * d, axis=-1, keepdims=True)
        y = d * lax.rsqrt(var + LN_EPS)
        g = g_ref[sl, :].astype(f32)
        o_ref[sl, :] = (y * (g * jax.nn.sigmoid(g))).astype(o_ref.dtype)
        return R * cd_b + kv

    lax.fori_loop(0, n_chunks, bwd, R0)


def _retention(proj3, log_gamma, cos, sin):
    B, S, _ = proj3.shape
    n_chunks = S // RET_CHUNK
    qb, kb = COL_RQ // RET_DK, COL_RK // RET_DK
    vb, gb = COL_RV // RET_DV, COL_RG // RET_DV
    grid_spec = pltpu.PrefetchScalarGridSpec(
        num_scalar_prefetch=1,
        grid=(B, RET_HEADS),
        in_specs=[
            pl.BlockSpec((None, S, RET_DK), lambda b, h, lg: (b, 0, qb + h)),
            pl.BlockSpec((None, S, RET_DK), lambda b, h, lg: (b, 0, kb + h)),
            pl.BlockSpec((None, S, RET_DV), lambda b, h, lg: (b, 0, vb + h)),
            pl.BlockSpec((None, S, RET_DV), lambda b, h, lg: (b, 0, gb + h)),
            pl.BlockSpec((S, RET_DK), lambda b, h, lg: (0, 0), pipeline_mode=pl.Buffered(1)),
            pl.BlockSpec((S, RET_DK), lambda b, h, lg: (0, 0), pipeline_mode=pl.Buffered(1)),
        ],
        out_specs=pl.BlockSpec((None, S, RET_DV), lambda b, h, lg: (b, 0, h)),
        scratch_shapes=[pltpu.VMEM((S, RET_DV), f32)],
    )
    return pl.pallas_call(
        functools.partial(_ret_kernel, n_chunks=n_chunks),
        grid_spec=grid_spec,
        out_shape=jax.ShapeDtypeStruct((B, S, RET_HEADS * RET_DV), bf16),
        compiler_params=_cparams(("parallel", "arbitrary")),
        name="retention",
    )(log_gamma, proj3, proj3, proj3, proj3, cos, sin)


ATT_Q = 128
ATT_W = ATT_Q + 2 * ATT_HALF


def _att_kernel(q_ref, k_ref, v_ref, bias_ref, o_ref, lse_ref, *, L):
    Q, W = ATT_Q, ATT_W
    scale = ATT_DH ** -0.5
    nt = (((1,), (1,)), ((), ()))

    def body(qb, carry):
        q0 = pl.multiple_of(qb * Q, Q)
        ks = pl.multiple_of(jnp.clip(q0 - ATT_HALF, 0, L - W), ATT_HALF)
        case = (q0 - ks) // ATT_HALF
        bias = bias_ref[case]
        q = (q_ref[pl.ds(q0, Q), :].astype(f32) * scale).astype(bf16)
        kw = k_ref[pl.ds(ks, W), :]
        vw = v_ref[pl.ds(ks, W), :]
        logits = lax.dot_general(q, kw, nt, preferred_element_type=f32) + bias
        m = jnp.max(logits, axis=-1, keepdims=True)
        p = jnp.exp(logits - m)
        den = jnp.sum(p, axis=-1, keepdims=True)
        o = jnp.dot(p.astype(bf16), vw, preferred_element_type=f32) / den
        o_ref[pl.ds(q0, Q), :] = o.astype(o_ref.dtype)
        lse_ref[pl.ds(q0, Q), :] = jnp.broadcast_to(m + jnp.log(den), (Q, ATT_DH))
        return carry

    lax.fori_loop(0, L // Q, body, 0)


def _attention_group(proj3, bias_tiles, gi, dil):
    B, S, _ = proj3.shape
    L = S // dil
    pv = proj3.reshape(B, L, dil * IN_W)
    nb = IN_W // ATT_DH
    qo = COL_AQ // ATT_DH + gi * ATT_HEADS
    ko = COL_AK // ATT_DH + gi * ATT_HEADS
    vo = COL_AV // ATT_DH + gi * ATT_HEADS
    out_w = ATT_HEADS * ATT_DH
    o, lse = pl.pallas_call(
        functools.partial(_att_kernel, L=L),
        grid=(B, dil, ATT_HEADS),
        in_specs=[
            pl.BlockSpec((None, L, ATT_DH), lambda b, r, j: (b, 0, r * nb + qo + j)),
            pl.BlockSpec((None, L, ATT_DH), lambda b, r, j: (b, 0, r * nb + ko + j)),
            pl.BlockSpec((None, L, ATT_DH), lambda b, r, j: (b, 0, r * nb + vo + j)),
            pl.BlockSpec((None, 3, ATT_Q, ATT_W), lambda b, r, j: (j, 0, 0, 0)),
        ],
        out_specs=[
            pl.BlockSpec((None, L, ATT_DH), lambda b, r, j: (b, 0, r * ATT_HEADS + j)),
            pl.BlockSpec((None, L, ATT_DH), lambda b, r, j: (b, 0, r * ATT_HEADS + j)),
        ],
        out_shape=[jax.ShapeDtypeStruct((B, L, dil * out_w), bf16),
                   jax.ShapeDtypeStruct((B, L, dil * out_w), f32)],
        compiler_params=_cparams(("parallel", "parallel", "arbitrary")),
        name=f"attention_g{gi}",
    )(pv, pv, pv, bias_tiles)
    return o.reshape(B * S, out_w), lse.reshape(B * S, out_w)


def _t5_bucket(rel):
    nb = NUM_BUCKETS // 2
    max_exact = nb // 2
    ret = jnp.where(rel > 0, nb, 0)
    n = jnp.abs(rel)
    nf = jnp.maximum(n, 1).astype(f32)
    large = max_exact + (jnp.log(nf / max_exact) / math.log(REL_MAX_DIST / max_exact)
                         * (nb - max_exact)).astype(jnp.int32)
    large = jnp.minimum(large, nb - 1)
    return ret + jnp.where(n < max_exact, n, large)


def _bias_tiles(rel_bias, gi, dil):
    i = jnp.arange(ATT_Q)[None, :, None]
    j = jnp.arange(ATT_W)[None, None, :]
    c = jnp.arange(3)[:, None, None]
    off = j - i - ATT_HALF * c
    tab = rel_bias[:, gi * ATT_HEADS:(gi + 1) * ATT_HEADS].astype(f32)
    bias = tab[_t5_bucket(off * dil)]
    bias = jnp.where((jnp.abs(off) <= ATT_HALF)[..., None], bias, NEG_INF)
    return bias.transpose(3, 0, 1, 2)


def _mix_kernel(ret_ref, o0_ref, o1_ref, o2_ref, l0_ref, l1_ref, l2_ref, ga_ref, gb_ref, x_ref,
                wr_ref, wa_ref, wo_ref, g_ref, b_ref, h_ref, hT_ref):
    l0, l1, l2 = l0_ref[...], l1_ref[...], l2_ref[...]
    m = jnp.maximum(jnp.maximum(l0, l1), l2)
    e0, e1, e2 = jnp.exp(l0 - m), jnp.exp(l1 - m), jnp.exp(l2 - m)
    den = e0 + e1 + e2
    att = (e0 / den * o0_ref[...].astype(f32) + e1 / den * o1_ref[...].astype(f32)
           + e2 / den * o2_ref[...].astype(f32))
    m1 = jnp.dot(ret_ref[...], wr_ref[...], preferred_element_type=f32)
    m2 = jnp.dot(att.astype(bf16), wa_ref[...], preferred_element_type=f32)
    merged = (jax.nn.sigmoid(ga_ref[...].astype(f32)) * m1
              + jax.nn.sigmoid(gb_ref[...].astype(f32)) * m2)
    y = jnp.dot(merged.astype(bf16), wo_ref[...], preferred_element_type=f32)
    h = _layernorm(DN_ALPHA * x_ref[...] + y, g_ref[...], b_ref[...])
    h_ref[...] = h
    hT_ref[...] = h.T.astype(bf16)


def _mix(ret, o_l, lse_l, proj2, x2, wr, wa, wo, g, b, tm=256):
    T = x2.shape[0]
    aw = ATT_HEADS * ATT_DH
    row = lambda w: pl.BlockSpec((tm, w), lambda i: (i, 0))
    return pl.pallas_call(
        _mix_kernel,
        grid=(T // tm,),
        in_specs=[row(D_MODEL), row(aw), row(aw), row(aw), row(aw), row(aw), row(aw),
                  pl.BlockSpec((tm, D_MODEL), lambda i: (i, COL_GA // D_MODEL)),
                  pl.BlockSpec((tm, D_MODEL), lambda i: (i, COL_GB // D_MODEL)),
                  row(D_MODEL),
                  _const_spec(wr.shape), _const_spec(wa.shape), _const_spec(wo.shape),
                  _const_spec(g.shape), _const_spec(b.shape)],
        out_specs=[pl.BlockSpec((tm, D_MODEL), lambda i: (i, 0)),
                   pl.BlockSpec((D_MODEL, tm), lambda i: (0, i))],
        out_shape=[jax.ShapeDtypeStruct((T, D_MODEL), f32),
                   jax.ShapeDtypeStruct((D_MODEL, T), bf16)],
        compiler_params=_cparams(("parallel",)),
        name="mix",
    )(ret, *o_l, *lse_l, proj2, proj2, x2, wr, wa, wo, g, b)


TOP_ROWS = 24


def _top_rows(s):
    tn = s.shape[1]
    rid = lax.broadcasted_iota(jnp.int32, (TOP_ROWS, tn), 0)
    top = jnp.full((TOP_ROWS, tn), -jnp.inf, f32)
    for it in range(PEER_TOPK + 1):
        m = jnp.max(s, axis=0, keepdims=True)
        top = jnp.where(rid == it, m, top)
        s = jnp.where(s == m, -jnp.inf, s)
    return top


def _peer_topk_kernel(hT_ref, wqT_ref, keys_ref, th1_ref, e1_ref, s2_ref, e2_ref):
    tn = hT_ref.shape[1]
    qT = jnp.dot(wqT_ref[...], hT_ref[...], preferred_element_type=f32).astype(bf16)
    rid8 = lax.broadcasted_iota(jnp.int32, (8, tn), 0)
    ninf = -jnp.inf
    for h in range(PEER_HEADS):
        s1 = jnp.dot(keys_ref[2 * h], qT[(2 * h) * 128:(2 * h + 1) * 128, :], preferred_element_type=f32)
        s2 = jnp.dot(keys_ref[2 * h + 1], qT[(2 * h + 1) * 128:(2 * h + 2) * 128, :],
                     preferred_element_type=f32)
        a = _top_rows(s1)
        b = _top_rows(s2)
        a8 = a[0:8]
        blocks = [a + b[0:1], a8 + b[1:2]]
        for k2, lim in ((2, 5), (3, 4), (4, 3), (5, 2), (6, 2), (7, 2)):
            blocks.append(jnp.where(rid8 < lim, a8 + b[k2:k2 + 1], ninf))
        blocks.append(b[8:TOP_ROWS] + a[0:1])
        cand = jnp.concatenate(blocks, axis=0)
        m0 = a[0:1] + b[0:1]
        z = jnp.zeros((1, tn), f32)
        m = m0
        m_prev = m0
        for it in range(PEER_TOPK + 1):
            m_prev = m
            m = jnp.max(cand, axis=0, keepdims=True)
            if it < PEER_TOPK:
                z = z + jnp.exp(m - m0)
            cand = jnp.where(cand == m, ninf, cand)
        tau = 0.5 * (m_prev + m)
        th1_ref[h] = tau - s1
        e1_ref[h] = jnp.exp(s1 - a[0:1]) / z
        s2_ref[h] = s2
        e2_ref[h] = jnp.exp(s2 - b[0:1])


def _peer_topk(hT, wqT, keys, tn=256):
    T = hT.shape[1]
    aux = jax.ShapeDtypeStruct((PEER_HEADS, PEER_NKEYS, T), f32)
    aux_spec = pl.BlockSpec((PEER_HEADS, PEER_NKEYS, tn), lambda i: (0, 0, i))
    return pl.pallas_call(
        _peer_topk_kernel,
        grid=(T // tn,),
        in_specs=[pl.BlockSpec((D_MODEL, tn), lambda i: (0, i)),
                  _const_spec(wqT.shape), _const_spec(keys.shape)],
        out_specs=[aux_spec] * 4,
        out_shape=[aux] * 4,
        compiler_params=_cparams(("parallel",)),
        name="peer_topk",
    )(hT, wqT, keys)


def _peer_dense_kernel(hT_ref, u_ref, vT_ref, th1_ref, e1_ref, s2_ref, e2_ref, o_ref, z_ref, *, eb):
    e = pl.program_id(1)
    nc = eb // PEER_NKEYS
    hu = jnp.dot(u_ref[...], hT_ref[...], preferred_element_type=f32)
    for c in range(nc):
        i1 = e * nc + c
        x = hu[c * PEER_NKEYS:(c + 1) * PEER_NKEYS, :]
        g = 0.5 * x * (1.0 + lax.erf(x * (2.0 ** -0.5)))
        w = None
        for h in range(PEER_HEADS):
            th = th1_ref[h, pl.ds(i1, 1), :]
            e1 = e1_ref[h, pl.ds(i1, 1), :]
            t = jnp.where(s2_ref[h] >= th, e2_ref[h] * e1, 0.0)
            w = t if w is None else w + t
        z_ref[c * PEER_NKEYS:(c + 1) * PEER_NKEYS, :] = (w * g).astype(bf16)
    contrib = jnp.dot(vT_ref[...], z_ref[...], preferred_element_type=f32)

    @pl.when(e == 0)
    def _():
        o_ref[...] = contrib

    @pl.when(e > 0)
    def _():
        o_ref[...] += contrib


def _peer_dense(hT, u, vT, aux, tn=512, eb=512):
    T = hT.shape[1]
    NE = u.shape[0]
    aux_spec = pl.BlockSpec((PEER_HEADS, PEER_NKEYS, tn), lambda i, e: (0, 0, i),
                            pipeline_mode=pl.Buffered(1))
    return pl.pallas_call(
        functools.partial(_peer_dense_kernel, eb=eb),
        grid=(T // tn, NE // eb),
        in_specs=[pl.BlockSpec((D_MODEL, tn), lambda i, e: (0, i)),
                  pl.BlockSpec((eb, D_MODEL), lambda i, e: (e, 0)),
                  pl.BlockSpec((D_MODEL, eb), lambda i, e: (0, e)),
                  aux_spec, aux_spec, aux_spec, aux_spec],
        out_specs=pl.BlockSpec((D_MODEL, tn), lambda i, e: (0, i)),
        out_shape=jax.ShapeDtypeStruct((D_MODEL, T), f32),
        scratch_shapes=[pltpu.VMEM((eb, tn), bf16)],
        compiler_params=_cparams(("parallel", "arbitrary")),
        name="peer_dense",
    )(hT, u, vT, *aux)


def _final_kernel(h_ref, ffT_ref, p_ref, wpe_ref, wpg_ref, g_ref, b_ref, o_ref):
    h = h_ref[...]
    ff = ffT_ref[...].T
    pe = (jnp.dot(p_ref[...].astype(bf16), wpe_ref[...], preferred_element_type=f32)
          * jax.nn.sigmoid(jnp.dot(h.astype(bf16), wpg_ref[...], preferred_element_type=f32)))
    o_ref[...] = _layernorm(DN_ALPHA * h + ff + pe, g_ref[...], b_ref[...])


def _final(h, ffT, p2, wpe, wpg, g, b, tm=512):
    T = h.shape[0]
    return pl.pallas_call(
        _final_kernel,
        grid=(T // tm,),
        in_specs=[pl.BlockSpec((tm, D_MODEL), lambda i: (i, 0)),
                  pl.BlockSpec((D_MODEL, tm), lambda i: (0, i)),
                  pl.BlockSpec((tm, P_DIM), lambda i: (i, 0)),
                  _const_spec(wpe.shape), _const_spec(wpg.shape),
                  _const_spec(g.shape), _const_spec(b.shape)],
        out_specs=pl.BlockSpec((tm, D_MODEL), lambda i: (i, 0)),
        out_shape=jax.ShapeDtypeStruct((T, D_MODEL), f32),
        compiler_params=_cparams(("parallel",)),
        name="final",
    )(h, ffT, p2, wpe, wpg, g, b)


def _rope_tables(S):
    half = RET_DK // 2
    inv = 1.0 / (ROPE_BASE ** jnp.linspace(0.0, 1.0, half, dtype=f32))
    ang = jnp.arange(S, dtype=f32)[:, None] * inv[None, :]
    cos, sin = jnp.cos(ang), jnp.sin(ang)
    return jnp.concatenate([cos, cos], axis=-1), jnp.concatenate([-sin, sin], axis=-1)


def _layer(x, p, w):
    B, S, _ = x.shape
    T = B * S
    x2 = x.reshape(T, D_MODEL)
    proj2 = _proj(x2, w["w_in"])
    proj3 = proj2.reshape(B, S, IN_W)
    cos, sin = _rope_tables(S)
    ret = _retention(proj3, w["log_gamma"], cos, sin).reshape(T, RET_HEADS * RET_DV)
    o_l, lse_l = [], []
    for gi, (_, dil) in enumerate(ATT_GROUPS):
        o, lse = _attention_group(proj3, w["bias_tiles"][gi], gi, dil)
        o_l.append(o)
        lse_l.append(lse)
    h, hT = _mix(ret, o_l, lse_l, proj2, x2, w["w_ret_o"], w["w_att_o"], w["w_out"], w["ln1_g"], w["ln1_b"])
    aux = _peer_topk(hT, w["wqT"], w["keys"])
    ffT = _peer_dense(hT, w["u"], w["vT"], aux)
    y = _final(h, ffT, p.reshape(T, P_DIM), w["w_pe"], w["w_pg"], w["ln2_g"], w["ln2_b"])
    return y.reshape(B, S, D_MODEL)


def kernel(x_prompt, x_sample, p_prompt, p_sample, w_in, ret_decay_logit, w_ret_o, w_att_o, w_out,
           rel_bias, ln1_g, ln1_b, peer_wq, peer_keys, peer_u, peer_v, w_pe, w_pg, ln2_g, ln2_b):
    y_prompt, y_sample = x_prompt, x_sample
    for i in range(DEPTH):
        wi = w_in[i]
        ref_cols = {"rq": (0, 1024), "rk": (1024, 2048), "rv": (2048, 4096), "rg": (4096, 6144),
                    "att": (6144, 15360), "ga": (15360, 17408), "gb": (17408, 19456)}
        order = ("ga", "gb", "rv", "rg", "rq", "rk", "att")
        w = {
            "w_in": jnp.concatenate([wi[:, ref_cols[n][0]:ref_cols[n][1]] for n in order], axis=1).astype(bf16),
            "log_gamma": jax.nn.log_sigmoid(ret_decay_logit[i].astype(f32)),
            "w_ret_o": w_ret_o[i].astype(bf16),
            "w_att_o": w_att_o[i].astype(bf16),
            "w_out": w_out[i].astype(bf16),
            "bias_tiles": [_bias_tiles(rel_bias, gi, dil) for gi, (_, dil) in enumerate(ATT_GROUPS)],
            "ln1_g": ln1_g[i].reshape(1, D_MODEL), "ln1_b": ln1_b[i].reshape(1, D_MODEL),
            "wqT": peer_wq[i].T.astype(bf16),
            "keys": peer_keys[i].reshape(2 * PEER_HEADS, PEER_NKEYS, PEER_NKEYS).astype(bf16),
            "u": peer_u[i].astype(bf16),
            "vT": peer_v[i].T.astype(bf16),
            "w_pe": w_pe[i].astype(bf16), "w_pg": w_pg[i].astype(bf16),
            "ln2_g": ln2_g[i].reshape(1, D_MODEL), "ln2_b": ln2_b[i].reshape(1, D_MODEL),
        }
        y_prompt = _layer(y_prompt, p_prompt[i], w)
        y_sample = _layer(y_sample, p_sample[i], w)
    return (y_prompt, y_sample)
```

```python
import functools
import math

import jax
import jax.numpy as jnp
from jax import lax
from jax.experimental import pallas as pl
from jax.experimental.pallas import tpu as pltpu

f32 = jnp.float32
bf16 = jnp.bfloat16

D_MODEL = 2048
DEPTH = 1
RET_HEADS = 8
RET_DK = 128
RET_DV = 256
RET_CHUNK = 128
ROPE_BASE = 10000.0
ATT_GROUPS = ((128, 1), (512, 4), (2048, 16))
ATT_HEADS = 8
ATT_DH = 128
ATT_HALF = 64
NUM_BUCKETS = 32
REL_MAX_DIST = 1024
PEER_HEADS = 8
PEER_NKEYS = 128
PEER_TOPK = 16
P_DIM = 256
DN_ALPHA = (2.0 * DEPTH) ** 0.25
LN_EPS = 1e-5
NEG_INF = -1e30

COL_GA, COL_GB, COL_RV, COL_RG, COL_RQ, COL_RK, COL_AQ, COL_AK, COL_AV = (
    0, 2048, 4096, 6144, 8192, 9216, 10240, 13312, 16384)
IN_W = 19456

VMEM_LIMIT = 56 * 1024 * 1024


def _cparams(sem):
    return pltpu.CompilerParams(dimension_semantics=sem, vmem_limit_bytes=VMEM_LIMIT)


def _const_spec(shape):
    nd = len(shape)
    return pl.BlockSpec(shape, lambda *_: (0,) * nd, pipeline_mode=pl.Buffered(1))


def _layernorm(y, g, b):
    mu = jnp.mean(y, axis=-1, keepdims=True)
    d = y - mu
    var = jnp.mean(d * d, axis=-1, keepdims=True)
    return d * lax.rsqrt(var + LN_EPS) * g + b


def _proj_kernel(x_ref, w_ref, o_ref):
    o_ref[...] = jnp.dot(x_ref[...].astype(bf16), w_ref[...],
                         preferred_element_type=f32).astype(o_ref.dtype)


def _proj(x2, w, tm=1024, tn=1024):
    M, K = x2.shape
    N = w.shape[1]
    return pl.pallas_call(
        _proj_kernel,
        grid=(M // tm, N // tn),
        in_specs=[pl.BlockSpec((tm, K), lambda i, j: (i, 0)),
                  pl.BlockSpec((K, tn), lambda i, j: (0, j))],
        out_specs=pl.BlockSpec((tm, tn), lambda i, j: (i, j)),
        out_shape=jax.ShapeDtypeStruct((M, N), bf16),
        compiler_params=_cparams(("parallel", "arbitrary")),
        name="proj",
    )(x2, w)


def _rot(t, cos, sin):
    return t * cos + pltpu.roll(t, RET_DK // 2, 1) * sin


def _ret_kernel(lg_ref, q_ref, k_ref, v_ref, g_ref, cos_ref, sin_ref, o_ref, acc_ref, *, n_chunks):
    C = RET_CHUNK
    h = pl.program_id(1)
    lgf = lg_ref[0, h]
    lgb = lg_ref[1, h]
    row = lax.broadcasted_iota(jnp.int32, (C, C), 0).astype(f32)
    col = lax.broadcasted_iota(jnp.int32, (C, C), 1).astype(f32)
    diff = row - col
    dec_f = jnp.where(diff >= 0, jnp.exp(lgf * jnp.maximum(diff, 0.0)), 0.0)
    dec_b = jnp.where(diff < 0, jnp.exp(lgb * jnp.maximum(-diff, 0.0)), 0.0)
    xi_f = jnp.exp(lgf * (row + 1.0))
    zeta_f = jnp.exp(lgf * (C - 1.0 - row))
    xi_b = jnp.exp(lgb * (C - row))
    zeta_b = jnp.exp(lgb * row)
    cd_f = jnp.exp(jnp.full((1, RET_DV), lgf * C, f32))
    cd_b = jnp.exp(jnp.full((1, RET_DV), lgb * C, f32))
    scale = RET_DK ** -0.5
    nt = (((1,), (1,)), ((), ()))

    def chunk(n, R, dec, xi, zeta):
        sl = pl.ds(pl.multiple_of(n * C, C), C)
        cos = cos_ref[sl, :]
        sin = sin_ref[sl, :]
        q = _rot(q_ref[sl, :].astype(f32), cos, sin)
        k = _rot(k_ref[sl, :].astype(f32), cos, sin) * scale
        v = v_ref[sl, :]
        s = lax.dot_general(q.astype(bf16), k.astype(bf16), nt, preferred_element_type=f32) * dec
        o = jnp.dot(s.astype(bf16), v, preferred_element_type=f32)
        o = o + jnp.dot((q * xi).astype(bf16), R.astype(bf16), preferred_element_type=f32)
        kz = (k * zeta).T.astype(bf16)
        kv = jnp.dot(kz, v, preferred_element_type=f32)
        return sl, o, kv

    def fwd(n, R):
        sl, o, kv = chunk(n, R, dec_f, xi_f, zeta_f)
        acc_ref[sl, :] = o
        return R * cd_f + kv

    R0 = jnp.zeros((RET_DK, RET_DV), f32)
    lax.fori_loop(0, n_chunks, fwd, R0)

    def bwd(i, R):
        n = n_chunks - 1 - i
        sl, o, kv = chunk(n, R, dec_b, xi_b, zeta_b)
        o = o + acc_ref[sl, :]
        mu = jnp.mean(o, axis=-1, keepdims=True)
        d = o - mu
        var = jnp.mean(d * d, axis=-1, keepdims=True)
        y = d * lax.rsqrt(var + LN_EPS)
        g = g_ref[sl, :].astype(f32)
        o_ref[sl, :] = (y * (g * jax.nn.sigmoid(g))).astype(o_ref.dtype)
        return R * cd_b + kv

    lax.fori_loop(0, n_chunks, bwd, R0)


def _retention(proj3, log_gamma, cos, sin):
    B, S, _ = proj3.shape
    n_chunks = S // RET_CHUNK
    qb, kb = COL_RQ // RET_DK, COL_RK // RET_DK
    vb, gb = COL_RV // RET_DV, COL_RG // RET_DV
    grid_spec = pltpu.PrefetchScalarGridSpec(
        num_scalar_prefetch=1,
        grid=(B, RET_HEADS),
        in_specs=[
            pl.BlockSpec((None, S, RET_DK), lambda b, h, lg: (b, 0, qb + h)),
            pl.BlockSpec((None, S, RET_DK), lambda b, h, lg: (b, 0, kb + h)),
            pl.BlockSpec((None, S, RET_DV), lambda b, h, lg: (b, 0, vb + h)),
            pl.BlockSpec((None, S, RET_DV), lambda b, h, lg: (b, 0, gb + h)),
            pl.BlockSpec((S, RET_DK), lambda b, h, lg: (0, 0), pipeline_mode=pl.Buffered(1)),
            pl.BlockSpec((S, RET_DK), lambda b, h, lg: (0, 0), pipeline_mode=pl.Buffered(1)),
        ],
        out_specs=pl.BlockSpec((None, S, RET_DV), lambda b, h, lg: (b, 0, h)),
        scratch_shapes=[pltpu.VMEM((S, RET_DV), f32)],
    )
    return pl.pallas_call(
        functools.partial(_ret_kernel, n_chunks=n_chunks),
        grid_spec=grid_spec,
        out_shape=jax.ShapeDtypeStruct((B, S, RET_HEADS * RET_DV), bf16),
        compiler_params=_cparams(("parallel", "arbitrary")),
        name="retention",
    )(log_gamma, proj3, proj3, proj3, proj3, cos, sin)


ATT_Q = 128
ATT_W = ATT_Q + 2 * ATT_HALF


def _att_kernel(q_ref, k_ref, v_ref, bias_ref, o_ref, lse_ref, *, L):
    Q, W = ATT_Q, ATT_W
    scale = ATT_DH ** -0.5
    nt = (((1,), (1,)), ((), ()))

    def body(qb, carry):
        q0 = pl.multiple_of(qb * Q, Q)
        ks = pl.multiple_of(jnp.clip(q0 - ATT_HALF, 0, L - W), ATT_HALF)
        case = (q0 - ks) // ATT_HALF
        bias = bias_ref[case]
        q = (q_ref[pl.ds(q0, Q), :].astype(f32) * scale).astype(bf16)
        kw = k_ref[pl.ds(ks, W), :]
        vw = v_ref[pl.ds(ks, W), :]
        logits = lax.dot_general(q, kw, nt, preferred_element_type=f32) + bias
        m = jnp.max(logits, axis=-1, keepdims=True)
        p = jnp.exp(logits - m)
        den = jnp.sum(p, axis=-1, keepdims=True)
        o = jnp.dot(p.astype(bf16), vw, preferred_element_type=f32) / den
        o_ref[pl.ds(q0, Q), :] = o.astype(o_ref.dtype)
        lse_ref[pl.ds(q0, Q), :] = jnp.broadcast_to(m + jnp.log(den), (Q, ATT_DH))
        return carry

    lax.fori_loop(0, L // Q, body, 0)


ATT_QD = 256


def _att_dil_kernel(q_ref, k_ref, v_ref, f_ref, o_ref, lse_ref, *, S, reach):
    Q = ATT_QD
    W = Q + 2 * reach
    scale = ATT_DH ** -0.5
    nt = (((1,), (1,)), ((), ()))

    def body(qb, carry):
        q0 = pl.multiple_of(qb * Q, Q)
        ks = pl.multiple_of(jnp.clip(q0 - reach, 0, S - W), Q)
        off = pl.multiple_of(2 * reach + ks - q0, Q)
        bias = f_ref[:, pl.ds(off, W)]
        q = (q_ref[pl.ds(q0, Q), :].astype(f32) * scale).astype(bf16)
        kw = k_ref[pl.ds(ks, W), :]
        vw = v_ref[pl.ds(ks, W), :]
        logits = lax.dot_general(q, kw, nt, preferred_element_type=f32) + bias
        m = jnp.max(logits, axis=-1, keepdims=True)
        p = jnp.exp(logits - m)
        den = jnp.sum(p, axis=-1, keepdims=True)
        o = jnp.dot(p.astype(bf16), vw, preferred_element_type=f32) / den
        o_ref[pl.ds(q0, Q), :] = o.astype(o_ref.dtype)
        lse_ref[pl.ds(q0, Q), :] = jnp.broadcast_to(m + jnp.log(den), (Q, ATT_DH))
        return carry

    lax.fori_loop(0, S // Q, body, 0)


def _attention_group(proj3, bias, gi, dil):
    B, S, _ = proj3.shape
    qo = COL_AQ // ATT_DH + gi * ATT_HEADS
    ko = COL_AK // ATT_DH + gi * ATT_HEADS
    vo = COL_AV // ATT_DH + gi * ATT_HEADS
    out_w = ATT_HEADS * ATT_DH
    if dil == 1:
        body = functools.partial(_att_kernel, L=S)
        bias_spec = pl.BlockSpec((None, 3, ATT_Q, ATT_W), lambda b, j: (j, 0, 0, 0))
    else:
        reach = ATT_HALF * dil
        assert reach % ATT_QD == 0 and S >= ATT_QD + 2 * reach
        body = functools.partial(_att_dil_kernel, S=S, reach=reach)
        bias_spec = pl.BlockSpec((None, ATT_QD, ATT_QD + 4 * reach), lambda b, j: (j, 0, 0))
    o, lse = pl.pallas_call(
        body,
        grid=(B, ATT_HEADS),
        in_specs=[
            pl.BlockSpec((None, S, ATT_DH), lambda b, j: (b, 0, qo + j)),
            pl.BlockSpec((None, S, ATT_DH), lambda b, j: (b, 0, ko + j)),
            pl.BlockSpec((None, S, ATT_DH), lambda b, j: (b, 0, vo + j)),
            bias_spec,
        ],
        out_specs=[
            pl.BlockSpec((None, S, ATT_DH), lambda b, j: (b, 0, j)),
            pl.BlockSpec((None, S, ATT_DH), lambda b, j: (b, 0, j)),
        ],
        out_shape=[jax.ShapeDtypeStruct((B, S, out_w), bf16),
                   jax.ShapeDtypeStruct((B, S, out_w), f32)],
        compiler_params=_cparams(("parallel", "arbitrary")),
        name=f"attention_g{gi}",
    )(proj3, proj3, proj3, bias)
    return o.reshape(B * S, out_w), lse.reshape(B * S, out_w)


def _t5_bucket(rel):
    nb = NUM_BUCKETS // 2
    max_exact = nb // 2
    ret = jnp.where(rel > 0, nb, 0)
    n = jnp.abs(rel)
    nf = jnp.maximum(n, 1).astype(f32)
    large = max_exact + (jnp.log(nf / max_exact) / math.log(REL_MAX_DIST / max_exact)
                         * (nb - max_exact)).astype(jnp.int32)
    large = jnp.minimum(large, nb - 1)
    return ret + jnp.where(n < max_exact, n, large)


def _bias_of_offset(rel_bias, gi, dil, rel):
    tab = rel_bias[:, gi * ATT_HEADS:(gi + 1) * ATT_HEADS].astype(f32)
    bucket = _t5_bucket(rel)[None]
    pad = (1,) * rel.ndim
    out = jnp.full((ATT_HEADS,) + rel.shape, NEG_INF, f32)
    for b in range(NUM_BUCKETS):
        out = jnp.where(bucket == b, tab[b].reshape((ATT_HEADS,) + pad), out)
    valid = (jnp.abs(rel) <= ATT_HALF * dil) & (rel % dil == 0)
    return jnp.where(valid[None], out, NEG_INF)


def _bias_tiles(rel_bias, gi, dil):
    if dil == 1:
        i = jnp.arange(ATT_Q)[None, :, None]
        j = jnp.arange(ATT_W)[None, None, :]
        c = jnp.arange(3)[:, None, None]
        return _bias_of_offset(rel_bias, gi, dil, j - i - ATT_HALF * c)
    reach = ATT_HALF * dil
    i = jnp.arange(ATT_QD)[:, None]
    jj = jnp.arange(ATT_QD + 4 * reach)[None, :]
    return _bias_of_offset(rel_bias, gi, dil, jj - i - 2 * reach)


def _mix_kernel(ret_ref, o0_ref, o1_ref, o2_ref, l0_ref, l1_ref, l2_ref, ga_ref, gb_ref, x_ref,
                wr_ref, wa_ref, wo_ref, g_ref, b_ref, h_ref, hT_ref):
    l0, l1, l2 = l0_ref[...], l1_ref[...], l2_ref[...]
    m = jnp.maximum(jnp.maximum(l0, l1), l2)
    e0, e1, e2 = jnp.exp(l0 - m), jnp.exp(l1 - m), jnp.exp(l2 - m)
    den = e0 + e1 + e2
    att = (e0 / den * o0_ref[...].astype(f32) + e1 / den * o1_ref[...].astype(f32)
           + e2 / den * o2_ref[...].astype(f32))
    m1 = jnp.dot(ret_ref[...], wr_ref[...], preferred_element_type=f32)
    m2 = jnp.dot(att.astype(bf16), wa_ref[...], preferred_element_type=f32)
    merged = (jax.nn.sigmoid(ga_ref[...].astype(f32)) * m1
              + jax.nn.sigmoid(gb_ref[...].astype(f32)) * m2)
    y = jnp.dot(merged.astype(bf16), wo_ref[...], preferred_element_type=f32)
    h = _layernorm(DN_ALPHA * x_ref[...] + y, g_ref[...], b_ref[...])
    h_ref[...] = h
    hT_ref[...] = h.T.astype(bf16)


def _mix(ret, o_l, lse_l, proj2, x2, wr, wa, wo, g, b, tm=256):
    T = x2.shape[0]
    aw = ATT_HEADS * ATT_DH
    row = lambda w: pl.BlockSpec((tm, w), lambda i: (i, 0))
    return pl.pallas_call(
        _mix_kernel,
        grid=(T // tm,),
        in_specs=[row(D_MODEL), row(aw), row(aw), row(aw), row(aw), row(aw), row(aw),
                  pl.BlockSpec((tm, D_MODEL), lambda i: (i, COL_GA // D_MODEL)),
                  pl.BlockSpec((tm, D_MODEL), lambda i: (i, COL_GB // D_MODEL)),
                  row(D_MODEL),
                  _const_spec(wr.shape), _const_spec(wa.shape), _const_spec(wo.shape),
                  _const_spec(g.shape), _const_spec(b.shape)],
        out_specs=[pl.BlockSpec((tm, D_MODEL), lambda i: (i, 0)),
                   pl.BlockSpec((D_MODEL, tm), lambda i: (0, i))],
        out_shape=[jax.ShapeDtypeStruct((T, D_MODEL), f32),
                   jax.ShapeDtypeStruct((D_MODEL, T), bf16)],
        compiler_params=_cparams(("parallel",)),
        name="mix",
    )(ret, *o_l, *lse_l, proj2, proj2, x2, wr, wa, wo, g, b)


TOP_ROWS = 24


def _top_rows(s):
    tn = s.shape[1]
    rid = lax.broadcasted_iota(jnp.int32, (TOP_ROWS, tn), 0)
    top = jnp.full((TOP_ROWS, tn), -jnp.inf, f32)
    for it in range(PEER_TOPK + 1):
        m = jnp.max(s, axis=0, keepdims=True)
        top = jnp.where(rid == it, m, top)
        s = jnp.where(s == m, -jnp.inf, s)
    return top


def _peer_topk_kernel(hT_ref, wqT_ref, keys_ref, th1_ref, e1_ref, s2_ref, e2_ref):
    tn = hT_ref.shape[1]
    qT = jnp.dot(wqT_ref[...], hT_ref[...], preferred_element_type=f32).astype(bf16)
    rid8 = lax.broadcasted_iota(jnp.int32, (8, tn), 0)
    ninf = -jnp.inf
    for h in range(PEER_HEADS):
        s1 = jnp.dot(keys_ref[2 * h], qT[(2 * h) * 128:(2 * h + 1) * 128, :], preferred_element_type=f32)
        s2 = jnp.dot(keys_ref[2 * h + 1], qT[(2 * h + 1) * 128:(2 * h + 2) * 128, :],
                     preferred_element_type=f32)
        a = _top_rows(s1)
        b = _top_rows(s2)
        a8 = a[0:8]
        blocks = [a + b[0:1], a8 + b[1:2]]
        for k2, lim in ((2, 5), (3, 4), (4, 3), (5, 2), (6, 2), (7, 2)):
            blocks.append(jnp.where(rid8 < lim, a8 + b[k2:k2 + 1], ninf))
        blocks.append(b[8:TOP_ROWS] + a[0:1])
        cand = jnp.concatenate(blocks, axis=0)
        m0 = a[0:1] + b[0:1]
        z = jnp.zeros((1, tn), f32)
        m = m0
        m_prev = m0
        for it in range(PEER_TOPK + 1):
            m_prev = m
            m = jnp.max(cand, axis=0, keepdims=True)
            if it < PEER_TOPK:
                z = z + jnp.exp(m - m0)
            cand = jnp.where(cand == m, ninf, cand)
        tau = 0.5 * (m_prev + m)
        th1_ref[h] = tau - s1
        e1_ref[h] = jnp.exp(s1 - a[0:1]) / z
        s2_ref[h] = s2
        e2_ref[h] = jnp.exp(s2 - b[0:1])


def _peer_topk(hT, wqT, keys, tn=256):
    T = hT.shape[1]
    aux = jax.ShapeDtypeStruct((PEER_HEADS, PEER_NKEYS, T), f32)
    aux_spec = pl.BlockSpec((PEER_HEADS, PEER_NKEYS, tn), lambda i: (0, 0, i))
    return pl.pallas_call(
        _peer_topk_kernel,
        grid=(T // tn,),
        in_specs=[pl.BlockSpec((D_MODEL, tn), lambda i: (0, i)),
                  _const_spec(wqT.shape), _const_spec(keys.shape)],
        out_specs=[aux_spec] * 4,
        out_shape=[aux] * 4,
        compiler_params=_cparams(("parallel",)),
        name="peer_topk",
    )(hT, wqT, keys)


PEER_EB = 512


def _peer_dense_kernel(hT_ref, u0_ref, ub_ref, ua_ref, vb_ref, va_ref, th1_ref, e1_ref, s2_ref, e2_ref,
                       o_ref, hua_ref, hub_ref, za_ref, zb0_ref, zb1_ref, *, n_steps):
    s = pl.program_id(1)
    nc = PEER_EB // PEER_NKEYS

    dr = D_MODEL // nc

    def scores(u_ref, hu_ref, c):
        rows = slice(c * PEER_NKEYS, (c + 1) * PEER_NKEYS)
        hu_ref[rows, :] = jnp.dot(u_ref[rows, :], hT_ref[...], preferred_element_type=f32)

    def contract(v_ref, z_ref, c):
        rows = slice(c * dr, (c + 1) * dr)
        o_ref[rows, :] += jnp.dot(v_ref[rows, :], z_ref[...], preferred_element_type=f32)

    def gated(blk, hu_ref, z_ref, c):
        i1 = blk * nc + c
        rows = slice(c * PEER_NKEYS, (c + 1) * PEER_NKEYS)
        x = hu_ref[rows, :]
        g = 0.5 * x * (1.0 + lax.erf(x * (2.0 ** -0.5)))
        w = None
        for h in range(PEER_HEADS):
            th = th1_ref[h, pl.ds(i1, 1), :]
            e1 = e1_ref[h, pl.ds(i1, 1), :]
            t = jnp.where(s2_ref[h] >= th, e2_ref[h] * e1, 0.0)
            w = t if w is None else w + t
        z_ref[rows, :] = (w * g).astype(bf16)

    @pl.when(s == 0)
    def _():
        o_ref[...] = jnp.zeros_like(o_ref)
        zb0_ref[...] = jnp.zeros_like(zb0_ref)
        for c in range(nc):
            scores(u0_ref, hua_ref, c)

    def main(zb_read, zb_write):
        for c in range(nc):
            scores(ub_ref, hub_ref, c)
            contract(vb_ref, zb_read, c)
            gated(2 * s, hua_ref, za_ref, c)
        for c in range(nc):
            scores(ua_ref, hua_ref, c)
            contract(va_ref, za_ref, c)
            gated(2 * s + 1, hub_ref, zb_write, c)

    @pl.when(jnp.logical_and(s < n_steps, s % 2 == 0))
    def _():
        main(zb0_ref, zb1_ref)

    @pl.when(jnp.logical_and(s < n_steps, s % 2 == 1))
    def _():
        main(zb1_ref, zb0_ref)

    @pl.when(s == n_steps)
    def _():
        for c in range(nc):
            contract(vb_ref, zb0_ref if n_steps % 2 == 0 else zb1_ref, c)


def _peer_dense(hT, u, vT, aux, tn=512):
    T = hT.shape[1]
    nb = u.shape[0] // PEER_EB
    n_steps = nb // 2
    last = nb - 1
    aux_spec = pl.BlockSpec((PEER_HEADS, PEER_NKEYS, tn), lambda i, s: (0, 0, i),
                            pipeline_mode=pl.Buffered(1))
    return pl.pallas_call(
        functools.partial(_peer_dense_kernel, n_steps=n_steps),
        grid=(T // tn, n_steps + 1),
        in_specs=[pl.BlockSpec((D_MODEL, tn), lambda i, s: (0, i)),
                  pl.BlockSpec((PEER_EB, D_MODEL), lambda i, s: (0, 0), pipeline_mode=pl.Buffered(1)),
                  pl.BlockSpec((PEER_EB, D_MODEL), lambda i, s: (jnp.minimum(2 * s + 1, last), 0)),
                  pl.BlockSpec((PEER_EB, D_MODEL), lambda i, s: (jnp.minimum(2 * s + 2, last), 0)),
                  pl.BlockSpec((D_MODEL, PEER_EB), lambda i, s: (0, jnp.maximum(2 * s - 1, 0))),
                  pl.BlockSpec((D_MODEL, PEER_EB), lambda i, s: (0, jnp.minimum(2 * s, last))),
                  aux_spec, aux_spec, aux_spec, aux_spec],
        out_specs=pl.BlockSpec((D_MODEL, tn), lambda i, s: (0, i)),
        out_shape=jax.ShapeDtypeStruct((D_MODEL, T), f32),
        scratch_shapes=[pltpu.VMEM((PEER_EB, tn), f32), pltpu.VMEM((PEER_EB, tn), f32),
                        pltpu.VMEM((PEER_EB, tn), bf16), pltpu.VMEM((PEER_EB, tn), bf16),
                        pltpu.VMEM((PEER_EB, tn), bf16)],
        compiler_params=_cparams(("parallel", "arbitrary")),
        name="peer_dense",
    )(hT, u, u, u, vT, vT, *aux)


def _final_kernel(h_ref, ffT_ref, p_ref, wpe_ref, wpg_ref, g_ref, b_ref, o_ref):
    h = h_ref[...]
    ff = ffT_ref[...].T
    pe = (jnp.dot(p_ref[...].astype(bf16), wpe_ref[...], preferred_element_type=f32)
          * jax.nn.sigmoid(jnp.dot(h.astype(bf16), wpg_ref[...], preferred_element_type=f32)))
    o_ref[...] = _layernorm(DN_ALPHA * h + ff + pe, g_ref[...], b_ref[...])


def _final(h, ffT, p2, wpe, wpg, g, b, tm=512):
    T = h.shape[0]
    return pl.pallas_call(
        _final_kernel,
        grid=(T // tm,),
        in_specs=[pl.BlockSpec((tm, D_MODEL), lambda i: (i, 0)),
                  pl.BlockSpec((D_MODEL, tm), lambda i: (0, i)),
                  pl.BlockSpec((tm, P_DIM), lambda i: (i, 0)),
                  _const_spec(wpe.shape), _const_spec(wpg.shape),
                  _const_spec(g.shape), _const_spec(b.shape)],
        out_specs=pl.BlockSpec((tm, D_MODEL), lambda i: (i, 0)),
        out_shape=jax.ShapeDtypeStruct((T, D_MODEL), f32),
        compiler_params=_cparams(("parallel",)),
        name="final",
    )(h, ffT, p2, wpe, wpg, g, b)


def _rope_tables(S):
    half = RET_DK // 2
    inv = 1.0 / (ROPE_BASE ** jnp.linspace(0.0, 1.0, half, dtype=f32))
    ang = jnp.arange(S, dtype=f32)[:, None] * inv[None, :]
    cos, sin = jnp.cos(ang), jnp.sin(ang)
    return jnp.concatenate([cos, cos], axis=-1), jnp.concatenate([-sin, sin], axis=-1)


def _layer(x, p, w):
    B, S, _ = x.shape
    T = B * S
    x2 = x.reshape(T, D_MODEL)
    proj2 = _proj(x2, w["w_in"])
    proj3 = proj2.reshape(B, S, IN_W)
    cos, sin = _rope_tables(S)
    ret = _retention(proj3, w["log_gamma"], cos, sin).reshape(T, RET_HEADS * RET_DV)
    o_l, lse_l = [], []
    for gi, (_, dil) in enumerate(ATT_GROUPS):
        o, lse = _attention_group(proj3, w["bias_tiles"][gi], gi, dil)
        o_l.append(o)
        lse_l.append(lse)
    h, hT = _mix(ret, o_l, lse_l, proj2, x2, w["w_ret_o"], w["w_att_o"], w["w_out"], w["ln1_g"], w["ln1_b"])
    aux = _peer_topk(hT, w["wqT"], w["keys"])
    ffT = _peer_dense(hT, w["u"], w["vT"], aux)
    y = _final(h, ffT, p.reshape(T, P_DIM), w["w_pe"], w["w_pg"], w["ln2_g"], w["ln2_b"])
    return y.reshape(B, S, D_MODEL)


def kernel(x_prompt, x_sample, p_prompt, p_sample, w_in, ret_decay_logit, w_ret_o, w_att_o, w_out,
           rel_bias, ln1_g, ln1_b, peer_wq, peer_keys, peer_u, peer_v, w_pe, w_pg, ln2_g, ln2_b):
    y_prompt, y_sample = x_prompt, x_sample
    for i in range(DEPTH):
        wi = w_in[i]
        ref_cols = {"rq": (0, 1024), "rk": (1024, 2048), "rv": (2048, 4096), "rg": (4096, 6144),
                    "att": (6144, 15360), "ga": (15360, 17408), "gb": (17408, 19456)}
        order = ("ga", "gb", "rv", "rg", "rq", "rk", "att")
        w = {
            "w_in": jnp.concatenate([wi[:, ref_cols[n][0]:ref_cols[n][1]] for n in order], axis=1).astype(bf16),
            "log_gamma": jax.nn.log_sigmoid(ret_decay_logit[i].astype(f32)),
            "w_ret_o": w_ret_o[i].astype(bf16),
            "w_att_o": w_att_o[i].astype(bf16),
            "w_out": w_out[i].astype(bf16),
            "bias_tiles": [_bias_tiles(rel_bias, gi, dil) for gi, (_, dil) in enumerate(ATT_GROUPS)],
            "ln1_g": ln1_g[i].reshape(1, D_MODEL), "ln1_b": ln1_b[i].reshape(1, D_MODEL),
            "wqT": peer_wq[i].T.astype(bf16),
            "keys": peer_keys[i].reshape(2 * PEER_HEADS, PEER_NKEYS, PEER_NKEYS).astype(bf16),
            "u": peer_u[i].astype(bf16),
            "vT": peer_v[i].T.astype(bf16),
            "w_pe": w_pe[i].astype(bf16), "w_pg": w_pg[i].astype(bf16),
            "ln2_g": ln2_g[i].reshape(1, D_MODEL), "ln2_b": ln2_b[i].reshape(1, D_MODEL),
        }
        y_prompt = _layer(y_prompt, p_prompt[i], w)
        y_sample = _layer(y_sample, p_sample[i], w)
    return (y_prompt, y_sample)
```

```python
import functools
import math

import jax
import jax.numpy as jnp
from jax import lax
from jax.experimental import pallas as pl
from jax.experimental.pallas import tpu as pltpu

f32 = jnp.float32
bf16 = jnp.bfloat16

D_MODEL = 2048
DEPTH = 1
RET_HEADS = 8
RET_DK = 128
RET_DV = 256
RET_CHUNK = 128
ROPE_BASE = 10000.0
ATT_GROUPS = ((128, 1), (512, 4), (2048, 16))
ATT_HEADS = 8
ATT_DH = 128
ATT_HALF = 64
NUM_BUCKETS = 32
REL_MAX_DIST = 1024
PEER_HEADS = 8
PEER_NKEYS = 128
PEER_TOPK = 16
P_DIM = 256
DN_ALPHA = (2.0 * DEPTH) ** 0.25
LN_EPS = 1e-5
NEG_INF = -1e30

COL_GA, COL_GB, COL_RV, COL_RG, COL_RQ, COL_RK, COL_AQ, COL_AK, COL_AV = (
    0, 2048, 4096, 6144, 8192, 9216, 10240, 13312, 16384)
IN_W = 19456

VMEM_LIMIT = 56 * 1024 * 1024


def _cparams(sem, flags=None):
    return pltpu.CompilerParams(dimension_semantics=sem, vmem_limit_bytes=VMEM_LIMIT, flags=flags)


def _const_spec(shape):
    nd = len(shape)
    return pl.BlockSpec(shape, lambda *_: (0,) * nd, pipeline_mode=pl.Buffered(1))


def _layernorm(y, g, b):
    mu = jnp.mean(y, axis=-1, keepdims=True)
    d = y - mu
    var = jnp.mean(d * d, axis=-1, keepdims=True)
    return d * lax.rsqrt(var + LN_EPS) * g + b


def _proj_kernel(x_ref, w_ref, o_ref):
    o_ref[...] = jnp.dot(x_ref[...].astype(bf16), w_ref[...],
                         preferred_element_type=f32).astype(o_ref.dtype)


def _proj(x2, w, tm=1024, tn=1024):
    M, K = x2.shape
    N = w.shape[1]
    return pl.pallas_call(
        _proj_kernel,
        grid=(M // tm, N // tn),
        in_specs=[pl.BlockSpec((tm, K), lambda i, j: (i, 0)),
                  pl.BlockSpec((K, tn), lambda i, j: (0, j))],
        out_specs=pl.BlockSpec((tm, tn), lambda i, j: (i, j)),
        out_shape=jax.ShapeDtypeStruct((M, N), bf16),
        compiler_params=_cparams(("parallel", "arbitrary")),
        name="proj",
    )(x2, w)


RET_UNROLL = 8


def _rot(t, cos, sin):
    return t * cos + pltpu.roll(t, RET_DK // 2, 1) * sin


def _ret_kernel(lg_ref, q_ref, k_ref, v_ref, g_ref, cos_ref, sin_ref, o_ref, acc_ref, *, n_chunks):
    C = RET_CHUNK
    h = pl.program_id(1)
    lgf = lg_ref[0, h]
    lgb = lg_ref[1, h]
    row = lax.broadcasted_iota(jnp.int32, (C, C), 0).astype(f32)
    col = lax.broadcasted_iota(jnp.int32, (C, C), 1).astype(f32)
    diff = row - col
    dec_f = jnp.where(diff >= 0, jnp.exp(lgf * jnp.maximum(diff, 0.0)), 0.0)
    dec_b = jnp.where(diff < 0, jnp.exp(lgb * jnp.maximum(-diff, 0.0)), 0.0)
    xi_f = jnp.exp(lgf * (row + 1.0))
    zeta_f = jnp.exp(lgf * (C - 1.0 - row))
    xi_b = jnp.exp(lgb * (C - row))
    zeta_b = jnp.exp(lgb * row)
    cd_f = jnp.exp(jnp.full((1, RET_DV), lgf * C, f32))
    cd_b = jnp.exp(jnp.full((1, RET_DV), lgb * C, f32))
    scale = RET_DK ** -0.5
    nt = (((1,), (1,)), ((), ()))

    def chunk(n, R, dec, xi, zeta):
        sl = pl.ds(pl.multiple_of(n * C, C), C)
        cos = cos_ref[sl, :]
        sin = sin_ref[sl, :]
        q = _rot(q_ref[sl, :].astype(f32), cos, sin)
        k = _rot(k_ref[sl, :].astype(f32), cos, sin) * scale
        v = v_ref[sl, :]
        s = lax.dot_general(q.astype(bf16), k.astype(bf16), nt, preferred_element_type=f32) * dec
        o = jnp.dot(s.astype(bf16), v, preferred_element_type=f32)
        o = o + jnp.dot((q * xi).astype(bf16), R.astype(bf16), preferred_element_type=f32)
        kz = (k * zeta).T.astype(bf16)
        kv = jnp.dot(kz, v, preferred_element_type=f32)
        return sl, o, kv

    def fwd(n, R):
        sl, o, kv = chunk(n, R, dec_f, xi_f, zeta_f)
        acc_ref[sl, :] = o
        return R * cd_f + kv

    R0 = jnp.zeros((RET_DK, RET_DV), f32)
    lax.fori_loop(0, n_chunks, fwd, R0, unroll=RET_UNROLL)

    def bwd(i, R):
        n = n_chunks - 1 - i
        sl, o, kv = chunk(n, R, dec_b, xi_b, zeta_b)
        o = o + acc_ref[sl, :]
        mu = jnp.mean(o, axis=-1, keepdims=True)
        d = o - mu
        var = jnp.mean(d * d, axis=-1, keepdims=True)
        y = d * lax.rsqrt(var + LN_EPS)
        g = g_ref[sl, :].astype(f32)
        o_ref[sl, :] = (y * (g * jax.nn.sigmoid(g))).astype(o_ref.dtype)
        return R * cd_b + kv

    lax.fori_loop(0, n_chunks, bwd, R0, unroll=RET_UNROLL)


def _retention(proj3, log_gamma, cos, sin):
    B, S, _ = proj3.shape
    n_chunks = S // RET_CHUNK
    qb, kb = COL_RQ // RET_DK, COL_RK // RET_DK
    vb, gb = COL_RV // RET_DV, COL_RG // RET_DV
    grid_spec = pltpu.PrefetchScalarGridSpec(
        num_scalar_prefetch=1,
        grid=(B, RET_HEADS),
        in_specs=[
            pl.BlockSpec((None, S, RET_DK), lambda b, h, lg: (b, 0, qb + h)),
            pl.BlockSpec((None, S, RET_DK), lambda b, h, lg: (b, 0, kb + h)),
            pl.BlockSpec((None, S, RET_DV), lambda b, h, lg: (b, 0, vb + h)),
            pl.BlockSpec((None, S, RET_DV), lambda b, h, lg: (b, 0, gb + h)),
            pl.BlockSpec((S, RET_DK), lambda b, h, lg: (0, 0), pipeline_mode=pl.Buffered(1)),
            pl.BlockSpec((S, RET_DK), lambda b, h, lg: (0, 0), pipeline_mode=pl.Buffered(1)),
        ],
        out_specs=pl.BlockSpec((None, S, RET_DV), lambda b, h, lg: (b, 0, h)),
        scratch_shapes=[pltpu.VMEM((S, RET_DV), f32)],
    )
    return pl.pallas_call(
        functools.partial(_ret_kernel, n_chunks=n_chunks),
        grid_spec=grid_spec,
        out_shape=jax.ShapeDtypeStruct((B, S, RET_HEADS * RET_DV), bf16),
        compiler_params=_cparams(("parallel", "arbitrary")),
        name="retention",
    )(log_gamma, proj3, proj3, proj3, proj3, cos, sin)


ATT_Q = 128
ATT_W = ATT_Q + 2 * ATT_HALF


def _att_kernel(q_ref, k_ref, v_ref, bias_ref, o_ref, lse_ref, *, L):
    Q, W = ATT_Q, ATT_W
    scale = ATT_DH ** -0.5
    nt = (((1,), (1,)), ((), ()))

    def body(qb, carry):
        q0 = pl.multiple_of(qb * Q, Q)
        ks = pl.multiple_of(jnp.clip(q0 - ATT_HALF, 0, L - W), ATT_HALF)
        case = (q0 - ks) // ATT_HALF
        bias = bias_ref[case]
        q = (q_ref[pl.ds(q0, Q), :].astype(f32) * scale).astype(bf16)
        kw = k_ref[pl.ds(ks, W), :]
        vw = v_ref[pl.ds(ks, W), :]
        logits = lax.dot_general(q, kw, nt, preferred_element_type=f32) + bias
        m = jnp.max(logits, axis=-1, keepdims=True)
        p = jnp.exp(logits - m)
        den = jnp.sum(p, axis=-1, keepdims=True)
        o = jnp.dot(p.astype(bf16), vw, preferred_element_type=f32) / den
        o_ref[pl.ds(q0, Q), :] = o.astype(o_ref.dtype)
        lse_ref[pl.ds(q0, Q), :] = jnp.broadcast_to(m + jnp.log(den), (Q, ATT_DH))
        return carry

    lax.fori_loop(0, L // Q, body, 0, unroll=8)


ATT_QD = 256
LANE = 128


def _att_dil_kernel(q_ref, k_ref, v_ref, f_ref, o_ref, lse_ref, *, S, reach):
    Q = ATT_QD
    W = Q + 2 * reach
    scale = ATT_DH ** -0.5
    nt = (((1,), (1,)), ((), ()))

    def body(qb, carry):
        q0 = pl.multiple_of(qb * Q, Q)
        ks = pl.multiple_of(jnp.clip(q0 - reach, 0, S - W), LANE)
        off = pl.multiple_of(2 * reach + ks - q0, LANE)
        bias = f_ref[:, pl.ds(off, W)]
        q = (q_ref[pl.ds(q0, Q), :].astype(f32) * scale).astype(bf16)
        kw = k_ref[pl.ds(ks, W), :]
        vw = v_ref[pl.ds(ks, W), :]
        logits = lax.dot_general(q, kw, nt, preferred_element_type=f32) + bias
        m = jnp.max(logits, axis=-1, keepdims=True)
        p = jnp.exp(logits - m)
        den = jnp.sum(p, axis=-1, keepdims=True)
        o = jnp.dot(p.astype(bf16), vw, preferred_element_type=f32) / den
        o_ref[pl.ds(q0, Q), :] = o.astype(o_ref.dtype)
        lse_ref[pl.ds(q0, Q), :] = jnp.broadcast_to(m + jnp.log(den), (Q, ATT_DH))
        return carry

    lax.fori_loop(0, S // Q, body, 0, unroll=2)


def _attention_group(proj3, bias, gi, dil):
    B, S, _ = proj3.shape
    qo = COL_AQ // ATT_DH + gi * ATT_HEADS
    ko = COL_AK // ATT_DH + gi * ATT_HEADS
    vo = COL_AV // ATT_DH + gi * ATT_HEADS
    out_w = ATT_HEADS * ATT_DH
    if dil == 1:
        body = functools.partial(_att_kernel, L=S)
        bias_spec = pl.BlockSpec((None, 3, ATT_Q, ATT_W), lambda b, j: (j, 0, 0, 0))
    else:
        reach = ATT_HALF * dil
        assert reach % LANE == 0 and S % ATT_QD == 0 and S >= ATT_QD + 2 * reach
        body = functools.partial(_att_dil_kernel, S=S, reach=reach)
        bias_spec = pl.BlockSpec((None, ATT_QD, ATT_QD + 4 * reach), lambda b, j: (j, 0, 0),
                                 pipeline_mode=pl.Buffered(1))
    o, lse = pl.pallas_call(
        body,
        grid=(B, ATT_HEADS),
        in_specs=[
            pl.BlockSpec((None, S, ATT_DH), lambda b, j: (b, 0, qo + j)),
            pl.BlockSpec((None, S, ATT_DH), lambda b, j: (b, 0, ko + j)),
            pl.BlockSpec((None, S, ATT_DH), lambda b, j: (b, 0, vo + j)),
            bias_spec,
        ],
        out_specs=[
            pl.BlockSpec((None, S, ATT_DH), lambda b, j: (b, 0, j)),
            pl.BlockSpec((None, S, ATT_DH), lambda b, j: (b, 0, j)),
        ],
        out_shape=[jax.ShapeDtypeStruct((B, S, out_w), bf16),
                   jax.ShapeDtypeStruct((B, S, out_w), f32)],
        compiler_params=_cparams(("parallel", "arbitrary")),
        name=f"attention_g{gi}",
    )(proj3, proj3, proj3, bias)
    return o.reshape(B * S, out_w), lse.reshape(B * S, out_w)


def _t5_bucket(rel):
    nb = NUM_BUCKETS // 2
    max_exact = nb // 2
    ret = jnp.where(rel > 0, nb, 0)
    n = jnp.abs(rel)
    nf = jnp.maximum(n, 1).astype(f32)
    large = max_exact + (jnp.log(nf / max_exact) / math.log(REL_MAX_DIST / max_exact)
                         * (nb - max_exact)).astype(jnp.int32)
    large = jnp.minimum(large, nb - 1)
    return ret + jnp.where(n < max_exact, n, large)


def _bias_of_offset(rel_bias, gi, dil, rel):
    tab = rel_bias[:, gi * ATT_HEADS:(gi + 1) * ATT_HEADS].astype(f32)
    bucket = _t5_bucket(rel)[None]
    pad = (1,) * rel.ndim
    out = jnp.full((ATT_HEADS,) + rel.shape, NEG_INF, f32)
    for b in range(NUM_BUCKETS):
        out = jnp.where(bucket == b, tab[b].reshape((ATT_HEADS,) + pad), out)
    valid = (jnp.abs(rel) <= ATT_HALF * dil) & (rel % dil == 0)
    return jnp.where(valid[None], out, NEG_INF)


def _bias_tiles(rel_bias, gi, dil):
    if dil == 1:
        i = jnp.arange(ATT_Q)[None, :, None]
        j = jnp.arange(ATT_W)[None, None, :]
        c = jnp.arange(3)[:, None, None]
        return _bias_of_offset(rel_bias, gi, dil, j - i - ATT_HALF * c)
    reach = ATT_HALF * dil
    i = jnp.arange(ATT_QD)[:, None]
    jj = jnp.arange(ATT_QD + 4 * reach)[None, :]
    return _bias_of_offset(rel_bias, gi, dil, jj - i - 2 * reach)


def _mix_kernel(ret_ref, o0_ref, o1_ref, o2_ref, l0_ref, l1_ref, l2_ref, ga_ref, gb_ref, x_ref,
                wr_ref, wa_ref, wo_ref, g_ref, b_ref, h_ref, hT_ref):
    l0, l1, l2 = l0_ref[...], l1_ref[...], l2_ref[...]
    m = jnp.maximum(jnp.maximum(l0, l1), l2)
    e0, e1, e2 = jnp.exp(l0 - m), jnp.exp(l1 - m), jnp.exp(l2 - m)
    den = e0 + e1 + e2
    att = (e0 / den * o0_ref[...].astype(f32) + e1 / den * o1_ref[...].astype(f32)
           + e2 / den * o2_ref[...].astype(f32))
    m1 = jnp.dot(ret_ref[...], wr_ref[...], preferred_element_type=f32)
    m2 = jnp.dot(att.astype(bf16), wa_ref[...], preferred_element_type=f32)
    merged = (jax.nn.sigmoid(ga_ref[...].astype(f32)) * m1
              + jax.nn.sigmoid(gb_ref[...].astype(f32)) * m2)
    y = jnp.dot(merged.astype(bf16), wo_ref[...], preferred_element_type=f32)
    h = _layernorm(DN_ALPHA * x_ref[...] + y, g_ref[...], b_ref[...])
    h_ref[...] = h
    hT_ref[...] = h.T.astype(bf16)


def _mix(ret, o_l, lse_l, proj2, x2, wr, wa, wo, g, b, tm=256):
    T = x2.shape[0]
    aw = ATT_HEADS * ATT_DH
    row = lambda w: pl.BlockSpec((tm, w), lambda i: (i, 0))
    return pl.pallas_call(
        _mix_kernel,
        grid=(T // tm,),
        in_specs=[row(D_MODEL), row(aw), row(aw), row(aw), row(aw), row(aw), row(aw),
                  pl.BlockSpec((tm, D_MODEL), lambda i: (i, COL_GA // D_MODEL)),
                  pl.BlockSpec((tm, D_MODEL), lambda i: (i, COL_GB // D_MODEL)),
                  row(D_MODEL),
                  _const_spec(wr.shape), _const_spec(wa.shape), _const_spec(wo.shape),
                  _const_spec(g.shape), _const_spec(b.shape)],
        out_specs=[pl.BlockSpec((tm, D_MODEL), lambda i: (i, 0)),
                   pl.BlockSpec((D_MODEL, tm), lambda i: (0, i))],
        out_shape=[jax.ShapeDtypeStruct((T, D_MODEL), f32),
                   jax.ShapeDtypeStruct((D_MODEL, T), bf16)],
        compiler_params=_cparams(("parallel",)),
        name="mix",
    )(ret, *o_l, *lse_l, proj2, proj2, x2, wr, wa, wo, g, b)


TOP_ROWS = 24
LOG2E = 1.4426950408889634


def _top_rows(s):
    tn = s.shape[1]
    rid = lax.broadcasted_iota(jnp.int32, (TOP_ROWS, tn), 0)
    top = jnp.full((TOP_ROWS, tn), -jnp.inf, f32)
    for it in range(PEER_TOPK + 1):
        m = jnp.max(s, axis=0, keepdims=True)
        top = jnp.where(rid == it, m, top)
        s = jnp.where(s == m, -jnp.inf, s)
    return top


def _peer_topk_kernel(hT_ref, wqT_ref, keys_ref, th_ref, c1_ref, s2c_ref):
    tn = hT_ref.shape[1]
    qT = jnp.dot(wqT_ref[...], hT_ref[...], preferred_element_type=f32).astype(bf16)
    rid8 = lax.broadcasted_iota(jnp.int32, (8, tn), 0)
    ninf = -jnp.inf
    for h in range(PEER_HEADS):
        s1 = jnp.dot(keys_ref[2 * h], qT[(2 * h) * 128:(2 * h + 1) * 128, :], preferred_element_type=f32)
        s2 = jnp.dot(keys_ref[2 * h + 1], qT[(2 * h + 1) * 128:(2 * h + 2) * 128, :],
                     preferred_element_type=f32)
        a = _top_rows(s1)
        b = _top_rows(s2)
        a8 = a[0:8]
        blocks = [a + b[0:1], a8 + b[1:2]]
        for k2, lim in ((2, 5), (3, 4), (4, 3), (5, 2), (6, 2), (7, 2)):
            blocks.append(jnp.where(rid8 < lim, a8 + b[k2:k2 + 1], ninf))
        blocks.append(b[8:TOP_ROWS] + a[0:1])
        cand = jnp.concatenate(blocks, axis=0)
        m0 = a[0:1] + b[0:1]
        z = jnp.zeros((1, tn), f32)
        m = m0
        m_prev = m0
        for it in range(PEER_TOPK + 1):
            m_prev = m
            m = jnp.max(cand, axis=0, keepdims=True)
            if it < PEER_TOPK:
                z = z + jnp.exp(m - m0)
            cand = jnp.where(cand == m, ninf, cand)
        tau = 0.5 * (m_prev + m)
        th_ref[h] = ((tau - b[0:1]) - s1) * LOG2E
        c1_ref[h] = (s1 - a[0:1] - jnp.log(z)) * LOG2E
        s2c_ref[h] = (s2 - b[0:1]) * LOG2E


def _peer_topk(hT, wqT, keys, tn=256):
    T = hT.shape[1]
    aux = jax.ShapeDtypeStruct((PEER_HEADS, PEER_NKEYS, T), f32)
    aux_spec = pl.BlockSpec((PEER_HEADS, PEER_NKEYS, tn), lambda i: (0, 0, i))
    return pl.pallas_call(
        _peer_topk_kernel,
        grid=(T // tn,),
        in_specs=[pl.BlockSpec((D_MODEL, tn), lambda i: (0, i)),
                  _const_spec(wqT.shape), _const_spec(keys.shape)],
        out_specs=[aux_spec] * 3,
        out_shape=[aux] * 3,
        compiler_params=_cparams(("parallel",)),
        name="peer_topk",
    )(hT, wqT, keys)


PEER_EB = 512


def _peer_dense_kernel(hT_ref, u0_ref, ub_ref, ua_ref, vb_ref, va_ref, th_ref, c1_ref, s2c_ref,
                       o_ref, hua_ref, hub_ref, za_ref, zb0_ref, zb1_ref, *, n_steps):
    s = pl.program_id(1)
    nc = PEER_EB // PEER_NKEYS

    dr = D_MODEL // nc

    def scores(u_ref, hu_ref, c):
        rows = slice(c * PEER_NKEYS, (c + 1) * PEER_NKEYS)
        hu_ref[rows, :] = jnp.dot(u_ref[rows, :], hT_ref[...], preferred_element_type=f32)

    def contract(v_ref, z_ref, c):
        rows = slice(c * dr, (c + 1) * dr)
        o_ref[rows, :] += jnp.dot(v_ref[rows, :], z_ref[...], preferred_element_type=f32)

    def gated(blk, hu_ref, z_ref, c):
        i1 = blk * nc + c
        rows = slice(c * PEER_NKEYS, (c + 1) * PEER_NKEYS)
        x = hu_ref[rows, :]
        g = 0.5 * x * (1.0 + lax.erf(x * (2.0 ** -0.5)))
        w = None
        for h in range(PEER_HEADS):
            th = th_ref[h, pl.ds(i1, 1), :]
            c1 = c1_ref[h, pl.ds(i1, 1), :]
            s2c = s2c_ref[h]
            t = jnp.where(s2c >= th, jnp.exp2(s2c + c1), 0.0)
            w = t if w is None else w + t
        z_ref[rows, :] = (w * g).astype(bf16)

    @pl.when(s == 0)
    def _():
        o_ref[...] = jnp.zeros_like(o_ref)
        zb0_ref[...] = jnp.zeros_like(zb0_ref)
        for c in range(nc):
            scores(u0_ref, hua_ref, c)

    def main(zb_read, zb_write):
        for c in range(nc):
            scores(ub_ref, hub_ref, c)
            contract(vb_ref, zb_read, c)
            gated(2 * s, hua_ref, za_ref, c)
        for c in range(nc):
            scores(ua_ref, hua_ref, c)
            contract(va_ref, za_ref, c)
            gated(2 * s + 1, hub_ref, zb_write, c)

    @pl.when(jnp.logical_and(s < n_steps, s % 2 == 0))
    def _():
        main(zb0_ref, zb1_ref)

    @pl.when(jnp.logical_and(s < n_steps, s % 2 == 1))
    def _():
        main(zb1_ref, zb0_ref)

    @pl.when(s == n_steps)
    def _():
        for c in range(nc):
            contract(vb_ref, zb0_ref if n_steps % 2 == 0 else zb1_ref, c)


def _peer_dense(hT, u, vT, aux, tn=512):
    T = hT.shape[1]
    nb = u.shape[0] // PEER_EB
    n_steps = nb // 2
    last = nb - 1
    aux_spec = pl.BlockSpec((PEER_HEADS, PEER_NKEYS, tn), lambda i, s: (0, 0, i),
                            pipeline_mode=pl.Buffered(1))
    return pl.pallas_call(
        functools.partial(_peer_dense_kernel, n_steps=n_steps),
        grid=(T // tn, n_steps + 1),
        in_specs=[pl.BlockSpec((D_MODEL, tn), lambda i, s: (0, i)),
                  pl.BlockSpec((PEER_EB, D_MODEL), lambda i, s: (0, 0), pipeline_mode=pl.Buffered(1)),
                  pl.BlockSpec((PEER_EB, D_MODEL), lambda i, s: (jnp.minimum(2 * s + 1, last), 0)),
                  pl.BlockSpec((PEER_EB, D_MODEL), lambda i, s: (jnp.minimum(2 * s + 2, last), 0)),
                  pl.BlockSpec((D_MODEL, PEER_EB), lambda i, s: (0, jnp.maximum(2 * s - 1, 0))),
                  pl.BlockSpec((D_MODEL, PEER_EB), lambda i, s: (0, jnp.minimum(2 * s, last))),
                  aux_spec, aux_spec, aux_spec],
        out_specs=pl.BlockSpec((D_MODEL, tn), lambda i, s: (0, i)),
        out_shape=jax.ShapeDtypeStruct((D_MODEL, T), f32),
        scratch_shapes=[pltpu.VMEM((PEER_EB, tn), f32), pltpu.VMEM((PEER_EB, tn), f32),
                        pltpu.VMEM((PEER_EB, tn), bf16), pltpu.VMEM((PEER_EB, tn), bf16),
                        pltpu.VMEM((PEER_EB, tn), bf16)],
        compiler_params=_cparams(("parallel", "arbitrary")),
        name="peer_dense",
    )(hT, u, u, u, vT, vT, *aux)


def _final_kernel(h_ref, ffT_ref, p_ref, wpe_ref, wpg_ref, g_ref, b_ref, o_ref):
    h = h_ref[...]
    ff = ffT_ref[...].T
    pe = (jnp.dot(p_ref[...].astype(bf16), wpe_ref[...], preferred_element_type=f32)
          * jax.nn.sigmoid(jnp.dot(h.astype(bf16), wpg_ref[...], preferred_element_type=f32)))
    o_ref[...] = _layernorm(DN_ALPHA * h + ff + pe, g_ref[...], b_ref[...])


def _final(h, ffT, p2, wpe, wpg, g, b, tm=512):
    T = h.shape[0]
    return pl.pallas_call(
        _final_kernel,
        grid=(T // tm,),
        in_specs=[pl.BlockSpec((tm, D_MODEL), lambda i: (i, 0)),
                  pl.BlockSpec((D_MODEL, tm), lambda i: (0, i)),
                  pl.BlockSpec((tm, P_DIM), lambda i: (i, 0)),
                  _const_spec(wpe.shape), _const_spec(wpg.shape),
                  _const_spec(g.shape), _const_spec(b.shape)],
        out_specs=pl.BlockSpec((tm, D_MODEL), lambda i: (i, 0)),
        out_shape=jax.ShapeDtypeStruct((T, D_MODEL), f32),
        compiler_params=_cparams(("parallel",)),
        name="final",
    )(h, ffT, p2, wpe, wpg, g, b)


def _rope_tables(S):
    half = RET_DK // 2
    inv = 1.0 / (ROPE_BASE ** jnp.linspace(0.0, 1.0, half, dtype=f32))
    ang = jnp.arange(S, dtype=f32)[:, None] * inv[None, :]
    cos, sin = jnp.cos(ang), jnp.sin(ang)
    return jnp.concatenate([cos, cos], axis=-1), jnp.concatenate([-sin, sin], axis=-1)


def _layer(x, p, w):
    B, S, _ = x.shape
    T = B * S
    x2 = x.reshape(T, D_MODEL)
    proj2 = _proj(x2, w["w_in"])
    proj3 = proj2.reshape(B, S, IN_W)
    cos, sin = _rope_tables(S)
    ret = _retention(proj3, w["log_gamma"], cos, sin).reshape(T, RET_HEADS * RET_DV)
    o_l, lse_l = [], []
    for gi, (_, dil) in enumerate(ATT_GROUPS):
        o, lse = _attention_group(proj3, w["bias_tiles"][gi], gi, dil)
        o_l.append(o)
        lse_l.append(lse)
    h, hT = _mix(ret, o_l, lse_l, proj2, x2, w["w_ret_o"], w["w_att_o"], w["w_out"], w["ln1_g"], w["ln1_b"])
    aux = _peer_topk(hT, w["wqT"], w["keys"])
    ffT = _peer_dense(hT, w["u"], w["vT"], aux)
    y = _final(h, ffT, p.reshape(T, P_DIM), w["w_pe"], w["w_pg"], w["ln2_g"], w["ln2_b"])
    return y.reshape(B, S, D_MODEL)


def kernel(x_prompt, x_sample, p_prompt, p_sample, w_in, ret_decay_logit, w_ret_o, w_att_o, w_out,
           rel_bias, ln1_g, ln1_b, peer_wq, peer_keys, peer_u, peer_v, w_pe, w_pg, ln2_g, ln2_b):
    y_prompt, y_sample = x_prompt, x_sample
    for i in range(DEPTH):
        wi = w_in[i]
        ref_cols = {"rq": (0, 1024), "rk": (1024, 2048), "rv": (2048, 4096), "rg": (4096, 6144),
                    "att": (6144, 15360), "ga": (15360, 17408), "gb": (17408, 19456)}
        order = ("ga", "gb", "rv", "rg", "rq", "rk", "att")
        w = {
            "w_in": jnp.concatenate([wi[:, ref_cols[n][0]:ref_cols[n][1]] for n in order], axis=1).astype(bf16),
            "log_gamma": jax.nn.log_sigmoid(ret_decay_logit[i].astype(f32)),
            "w_ret_o": w_ret_o[i].astype(bf16),
            "w_att_o": w_att_o[i].astype(bf16),
            "w_out": w_out[i].astype(bf16),
            "bias_tiles": [_bias_tiles(rel_bias, gi, dil) for gi, (_, dil) in enumerate(ATT_GROUPS)],
            "ln1_g": ln1_g[i].reshape(1, D_MODEL), "ln1_b": ln1_b[i].reshape(1, D_MODEL),
            "wqT": peer_wq[i].T.astype(bf16),
            "keys": peer_keys[i].reshape(2 * PEER_HEADS, PEER_NKEYS, PEER_NKEYS).astype(bf16),
            "u": peer_u[i].astype(bf16),
            "vT": peer_v[i].T.astype(bf16),
            "w_pe": w_pe[i].astype(bf16), "w_pg": w_pg[i].astype(bf16),
            "ln2_g": ln2_g[i].reshape(1, D_MODEL), "ln2_b": ln2_b[i].reshape(1, D_MODEL),
        }
        y_prompt = _layer(y_prompt, p_prompt[i], w)
        y_sample = _layer(y_sample, p_sample[i], w)
    return (y_prompt, y_sample)
```
